```python
import math
import jax
import jax.numpy as jnp
from jax import lax
import numpy as np

D_MODEL = 1024
BATCH = 4
SEQ = 4096
DEPTH = 4
DEC_BATCH = 128
DEC_SEQ = 1
PAST_LEN = 8192
PAGE_SIZE = 128

A_HEADS = 4
A_KV_HEADS = 1
A_GROUP = A_HEADS // A_KV_HEADS
A_QK_DIM = 64
A_V_DIM = 2 * A_QK_DIM
B_HEADS = 4
B_Q_RANK = 256
B_KV_RANK = 128
B_NOPE_DIM = 64
B_ROPE_DIM = 32
B_V_DIM = 64
ROPE_THETA = 10000.0
C_HEADS = 4
C_KV_HEADS = 1
C_GROUP = C_HEADS // C_KV_HEADS
C_HEAD_DIM = 64
MOBA_BLOCK = 256
MOBA_TOPK = 3
A_WIDTH = A_HEADS * A_V_DIM
B_WIDTH = B_HEADS * B_V_DIM
C_WIDTH = C_HEADS * C_HEAD_DIM
MIX_WIDTH = A_WIDTH + B_WIDTH + C_WIDTH
IN_SIZES = (A_HEADS * 2 * A_QK_DIM, A_KV_HEADS * 2 * A_QK_DIM, A_KV_HEADS * A_V_DIM,
            B_Q_RANK, B_KV_RANK, B_ROPE_DIM,
            C_HEADS * C_HEAD_DIM, C_KV_HEADS * C_HEAD_DIM, C_KV_HEADS * C_HEAD_DIM)
IN_COLS = sum(IN_SIZES)
N_MEM = 256
CROSS_HEADS = 4
CROSS_HEAD_DIM = 64
D_FF = 2816
CONV_WIDTH = 3
LN_EPS = 1e-5
RMS_EPS = 1e-6
DEEPNORM_ALPHA = (2 * DEPTH) ** 0.25
DEEPNORM_BETA = (8 * DEPTH) ** -0.25
Q_BLOCK = 128
MOBA_Q_BLOCK = 64
NEG_INF = -1e30

kernel_name = "hymba_diff_mla_moba_decoder_step"

F32 = jnp.float32


def layer_norm(x, g, b):
    xf = x.astype(F32)
    mu = jnp.mean(xf, axis=-1, keepdims=True)
    var = jnp.mean(jnp.square(xf - mu), axis=-1, keepdims=True)
    return ((xf - mu) * lax.rsqrt(var + LN_EPS) * g + b).astype(x.dtype)


def rms_norm(x, g):
    xf = x.astype(F32)
    return (xf * lax.rsqrt(jnp.mean(xf * xf, axis=-1, keepdims=True) + RMS_EPS) * g).astype(x.dtype)


def rope(x, pos):
    half = x.shape[-1] // 2
    freqs = ROPE_THETA ** (-jnp.arange(half, dtype=F32) / half)
    ang = pos.astype(F32)[:, None] * freqs[None, :]
    shape = (1, pos.shape[0]) + (1,) * (x.ndim - 3) + (half,)
    cos, sin = jnp.cos(ang).reshape(shape), jnp.sin(ang).reshape(shape)
    xf = x.astype(F32)
    x1, x2 = xf[..., :half], xf[..., half:]
    return jnp.concatenate([x1 * cos - x2 * sin, x1 * sin + x2 * cos], axis=-1).astype(x.dtype)


def alibi_slopes():
    n = A_HEADS + C_HEADS
    s = 2.0 ** (-8.0 * np.arange(1, n + 1) / n)
    return jnp.asarray(s[0::2], F32), jnp.asarray(s[1::2], F32)


def gather_pages(pool, page_table):
    g = pool[page_table]
    return g.reshape(g.shape[0], -1, *g.shape[3:])


def sweep_query_blocks(fn, q, pos_q, block):
    t = pos_q.shape[0]
    if t <= block or t % block:
        return fn(q, pos_q)
    nb = t // block
    qs = jnp.moveaxis(q.reshape(q.shape[0], nb, block, *q.shape[2:]), 1, 0)
    out = lax.map(lambda a: fn(a[0], a[1]), (qs, pos_q.reshape(nb, block)))
    out = jnp.moveaxis(out, 0, 1)
    return out.reshape(out.shape[0], t, *out.shape[3:])


def diff_attn_block(q, pos_q, k, v, pos_k, lam, slopes):
    s = jnp.einsum('bckgmd,blkmd->bkgmcl', q, k, preferred_element_type=F32) / math.sqrt(A_QK_DIM)
    dist = pos_q[:, None] - pos_k[None, :]
    s = s - slopes.reshape(1, A_KV_HEADS, A_GROUP, 1, 1, 1) * dist.astype(F32)
    s = jnp.where(dist >= 0, s, NEG_INF)
    p = jax.nn.softmax(s, axis=-1)
    p = p[:, :, :, 0] - lam * p[:, :, :, 1]
    return jnp.einsum('bkgcl,blkv->bckgv', p, v.astype(F32))


def latent_attn_block(q, pos_q, k, v, pos_k):
    s = jnp.einsum('bche,ble->bhcl', q, k, preferred_element_type=F32) / math.sqrt(B_NOPE_DIM + B_ROPE_DIM)
    s = jnp.where(pos_q[:, None] >= pos_k[None, :], s, NEG_INF)
    p = jax.nn.softmax(s, axis=-1)
    return jnp.einsum('bhcl,blr->bchr', p, v.astype(F32))


def moba_block(q, pos_q, kb, vb, k_mean, slopes):
    nb, c = q.shape[0], q.shape[1]
    n_blk = kb.shape[2]
    gate = jnp.einsum('bckgd,bknd->bckgn', q, k_mean, preferred_element_type=F32)
    own = pos_q // MOBA_BLOCK
    fully_past = jnp.arange(n_blk)[None, :] < own[:, None]
    gate = jnp.where(fully_past[None, :, None, None, :], gate, NEG_INF)
    n_sel = min(MOBA_TOPK, n_blk)
    top = lax.top_k(gate, n_sel)[1]
    own_idx = jnp.broadcast_to(own[None, :, None, None, None], top.shape[:-1] + (1,)).astype(top.dtype)
    idx = jnp.concatenate([top, own_idx], axis=-1)
    slot_ok = jnp.concatenate([jnp.arange(n_sel)[None, :] < own[:, None],
                               jnp.ones((c, 1), dtype=bool)], axis=-1)
    bi = jnp.arange(nb)[:, None, None, None, None]
    hi = jnp.arange(C_KV_HEADS)[None, None, :, None, None]
    kg = kb[bi, hi, idx]
    vg = vb[bi, hi, idx]
    kpos = idx[..., None] * MOBA_BLOCK + jnp.arange(MOBA_BLOCK)
    dist = pos_q[None, :, None, None, None, None] - kpos
    valid = slot_ok[None, :, None, None, :, None] & (dist >= 0)
    s = jnp.einsum('bckgd,bckgsjd->bckgsj', q, kg, preferred_element_type=F32) / math.sqrt(C_HEAD_DIM)
    s = s - slopes.reshape(1, 1, C_KV_HEADS, C_GROUP, 1, 1) * dist.astype(F32)
    s = jnp.where(valid, s, NEG_INF)
    shp = s.shape
    p = jax.nn.softmax(s.reshape(shp[:4] + (-1,)), axis=-1).reshape(shp)
    return jnp.einsum('bckgsj,bckgsjd->bckgd', p, vg.astype(F32))


def mixing_sublayer(x, pos, past, w_in, w_uq, g_cq, g_ckv, w_uk, w_uv,
                    lam_q1, lam_k1, lam_q2, lam_k2, g_diff, w_out, layer):
    nb, t = x.shape[0], x.shape[1]
    h = jnp.einsum('btd,de->bte', x, w_in)
    a_q, a_k, a_v, b_cq, b_ckv, b_kpe, c_q, c_k, c_v = jnp.split(h, list(np.cumsum(IN_SIZES)[:-1]), axis=-1)
    a_k = a_k.reshape(nb, t, A_KV_HEADS, 2 * A_QK_DIM)
    a_v = a_v.reshape(nb, t, A_KV_HEADS, A_V_DIM)
    b_lat = rms_norm(b_ckv, g_ckv)
    b_pe = rope(b_kpe, pos)
    c_k = c_k.reshape(nb, t, C_KV_HEADS, C_HEAD_DIM)
    c_v = c_v.reshape(nb, t, C_KV_HEADS, C_HEAD_DIM)
    rows = (a_k, a_v, b_lat, b_pe, c_k, c_v)
    if past is None:
        full = rows
    else:
        full = tuple(jnp.concatenate([p.astype(r.dtype), r], axis=1) for p, r in zip(past, rows))
    fa_k, fa_v, fb_lat, fb_pe, fc_k, fc_v = full
    n_keys = fa_k.shape[1]
    pos_k = jnp.arange(n_keys, dtype=jnp.int32)
    slopes_a, slopes_c = alibi_slopes()

    lam_init = 0.8 - 0.6 * math.exp(-0.3 * layer)
    lam = (jnp.exp(jnp.sum(lam_q1.astype(F32) * lam_k1.astype(F32)))
           - jnp.exp(jnp.sum(lam_q2.astype(F32) * lam_k2.astype(F32))) + lam_init)
    qa = a_q.reshape(nb, t, A_KV_HEADS, A_GROUP, 2, A_QK_DIM)
    ka = fa_k.reshape(nb, n_keys, A_KV_HEADS, 2, A_QK_DIM)
    oa = sweep_query_blocks(lambda q, pq: diff_attn_block(q, pq, ka, fa_v, pos_k, lam, slopes_a), qa, pos, Q_BLOCK)
    oa = rms_norm(oa.reshape(nb, t, A_HEADS, A_V_DIM), g_diff) * (1.0 - lam_init)
    oa = oa.reshape(nb, t, A_WIDTH).astype(x.dtype)

    qb = jnp.einsum('btr,rhe->bthe', rms_norm(b_cq, g_cq), w_uq)
    q_pe = rope(qb[..., B_NOPE_DIM:], pos)
    q_lat = jnp.einsum('bthn,rhn->bthr', qb[..., :B_NOPE_DIM], w_uk)
    qb_cat = jnp.concatenate([q_lat, q_pe], axis=-1)
    kb_cat = jnp.concatenate([fb_lat, fb_pe], axis=-1)
    ob = sweep_query_blocks(lambda q, pq: latent_attn_block(q, pq, kb_cat, fb_lat, pos_k), qb_cat, pos, Q_BLOCK)
    ob = jnp.einsum('bthr,rhv->bthv', ob.astype(x.dtype), w_uv).reshape(nb, t, B_WIDTH)

    pad = (-n_keys) % MOBA_BLOCK
    n_blk = (n_keys + pad) // MOBA_BLOCK

    def to_blocks(a):
        a = jnp.pad(a, ((0, 0), (0, pad), (0, 0), (0, 0)))
        return a.reshape(nb, n_blk, MOBA_BLOCK, C_KV_HEADS, C_HEAD_DIM).transpose(0, 3, 1, 2, 4)

    kc, vc = to_blocks(fc_k), to_blocks(fc_v)
    kc_mean = jnp.mean(kc.astype(F32), axis=3)
    qc = c_q.reshape(nb, t, C_KV_HEADS, C_GROUP, C_HEAD_DIM)
    oc = sweep_query_blocks(lambda q, pq: moba_block(q, pq, kc, vc, kc_mean, slopes_c), qc, pos, MOBA_Q_BLOCK)
    oc = oc.reshape(nb, t, C_WIDTH).astype(x.dtype)

    out = jnp.einsum('bte,ed->btd', jnp.concatenate([oa, ob, oc], axis=-1), w_out)
    return out, rows


def cross_sublayer(x, mem_k, mem_v, w_q, w_o):
    q = jnp.einsum('btd,dhe->bthe', x, w_q)
    s = jnp.einsum('bthe,bmhe->bhtm', q, mem_k, preferred_element_type=F32) / math.sqrt(CROSS_HEAD_DIM)
    p = jax.nn.softmax(s, axis=-1)
    o = jnp.einsum('bhtm,bmhe->bthe', p, mem_v.astype(F32)).astype(x.dtype)
    return jnp.einsum('bthe,hed->btd', o, w_o)


def conv_glu(x, conv_prev, w_up, conv_w, conv_b, w_down):
    u = jnp.einsum('btd,df->btf', x, w_up)
    g, val = jnp.split(u, 2, axis=-1)
    gp = jnp.concatenate([conv_prev.astype(g.dtype), g], axis=1)
    gc = lax.conv_general_dilated(gp, conv_w.astype(g.dtype)[:, None, :], window_strides=(1,),
                                  padding='VALID', dimension_numbers=('NWC', 'WIO', 'NWC'),
                                  feature_group_count=D_FF) + conv_b
    y = jax.nn.gelu(gc, approximate=False) * val
    return jnp.einsum('btf,fd->btd', y, w_down), gp[:, -(CONV_WIDTH - 1):]


def setup_inputs(seed: int = 0) -> dict:
    key = jax.random.key(seed)
    ks = iter(jax.random.split(key, 48))

    def nrm(shape, scale=1.0):
        return scale * jax.random.normal(next(ks), shape, F32)

    n_pages = PAST_LEN // PAGE_SIZE
    n_used = DEC_BATCH * n_pages
    n_pool = n_used + max(1, n_used // 4)
    page_table = jax.random.permutation(next(ks), n_pool)[:n_used].reshape(DEC_BATCH, n_pages).astype(jnp.int32)
    beta = DEEPNORM_BETA
    return {
        "x_prompt": nrm((BATCH, SEQ, D_MODEL)),
        "x_sample": nrm((DEC_BATCH, DEC_SEQ, D_MODEL)),
        "cache_a_k": nrm((DEPTH, n_pool, PAGE_SIZE, A_KV_HEADS, 2 * A_QK_DIM)),
        "cache_a_v": nrm((DEPTH, n_pool, PAGE_SIZE, A_KV_HEADS, A_V_DIM)),
        "cache_b_latent": nrm((DEPTH, n_pool, PAGE_SIZE, B_KV_RANK)),
        "cache_b_rope": nrm((DEPTH, n_pool, PAGE_SIZE, B_ROPE_DIM)),
        "cache_c_k": nrm((DEPTH, n_pool, PAGE_SIZE, C_KV_HEADS, C_HEAD_DIM)),
        "cache_c_v": nrm((DEPTH, n_pool, PAGE_SIZE, C_KV_HEADS, C_HEAD_DIM)),
        "cache_mem_k": nrm((DEPTH, DEC_BATCH, N_MEM, CROSS_HEADS, CROSS_HEAD_DIM)),
        "cache_mem_v": nrm((DEPTH, DEC_BATCH, N_MEM, CROSS_HEADS, CROSS_HEAD_DIM)),
        "state_ffn_conv": nrm((DEPTH, DEC_BATCH, CONV_WIDTH - 1, D_FF)),
        "page_table": page_table,
        "mem_prompt": nrm((BATCH, N_MEM, D_MODEL)),
        "w_in": nrm((DEPTH, D_MODEL, IN_COLS), D_MODEL ** -0.5),
        "w_uq": nrm((DEPTH, B_Q_RANK, B_HEADS, B_NOPE_DIM + B_ROPE_DIM), B_Q_RANK ** -0.5),
        "g_cq": 1.0 + nrm((DEPTH, B_Q_RANK), 0.02),
        "g_ckv": 1.0 + nrm((DEPTH, B_KV_RANK), 0.02),
        "w_uk": nrm((DEPTH, B_KV_RANK, B_HEADS, B_NOPE_DIM), B_KV_RANK ** -0.5),
        "w_uv": nrm((DEPTH, B_KV_RANK, B_HEADS, B_V_DIM), B_KV_RANK ** -0.5),
        "lam_q1": nrm((DEPTH, A_QK_DIM), 0.1),
        "lam_k1": nrm((DEPTH, A_QK_DIM), 0.1),
        "lam_q2": nrm((DEPTH, A_QK_DIM), 0.1),
        "lam_k2": nrm((DEPTH, A_QK_DIM), 0.1),
        "g_diff": 1.0 + nrm((DEPTH, A_V_DIM), 0.02),
        "w_out": nrm((DEPTH, MIX_WIDTH, D_MODEL), beta * MIX_WIDTH ** -0.5),
        "w_xq": nrm((DEPTH, D_MODEL, CROSS_HEADS, CROSS_HEAD_DIM), D_MODEL ** -0.5),
        "w_xk": nrm((DEPTH, D_MODEL, CROSS_HEADS, CROSS_HEAD_DIM), D_MODEL ** -0.5),
        "w_xv": nrm((DEPTH, D_MODEL, CROSS_HEADS, CROSS_HEAD_DIM), D_MODEL ** -0.5),
        "w_xo": nrm((DEPTH, CROSS_HEADS, CROSS_HEAD_DIM, D_MODEL), beta * (CROSS_HEADS * CROSS_HEAD_DIM) ** -0.5),
        "w_up": nrm((DEPTH, D_MODEL, 2 * D_FF), D_MODEL ** -0.5),
        "conv_w": nrm((DEPTH, CONV_WIDTH, D_FF), CONV_WIDTH ** -0.5),
        "conv_b": nrm((DEPTH, D_FF), 0.02),
        "w_down": nrm((DEPTH, D_FF, D_MODEL), beta * D_FF ** -0.5),
        "ln_g": 1.0 + nrm((DEPTH, 3, D_MODEL), 0.02),
        "ln_b": nrm((DEPTH, 3, D_MODEL), 0.02),
    }


def reference(x_prompt, x_sample, cache_a_k, cache_a_v, cache_b_latent, cache_b_rope, cache_c_k, cache_c_v,
              cache_mem_k, cache_mem_v, state_ffn_conv, page_table, mem_prompt,
              w_in, w_uq, g_cq, g_ckv, w_uk, w_uv, lam_q1, lam_k1, lam_q2, lam_k2, g_diff, w_out,
              w_xq, w_xk, w_xv, w_xo, w_up, conv_w, conv_b, w_down, ln_g, ln_b):
    pos_p = jnp.arange(SEQ, dtype=jnp.int32)
    pos_s = PAST_LEN + jnp.arange(DEC_SEQ, dtype=jnp.int32)
    xp, xs = x_prompt, x_sample
    conv0 = jnp.zeros((BATCH, CONV_WIDTH - 1, D_FF), x_prompt.dtype)
    paged = (cache_a_k, cache_a_v, cache_b_latent, cache_b_rope, cache_c_k, cache_c_v)
    rows_p = [[] for _ in paged]
    rows_s = [[] for _ in paged]
    memk_p, memv_p, conv_p, conv_s = [], [], [], []
    for l in range(DEPTH):
        mix_w = (w_in[l], w_uq[l], g_cq[l], g_ckv[l], w_uk[l], w_uv[l],
                 lam_q1[l], lam_k1[l], lam_q2[l], lam_k2[l], g_diff[l], w_out[l])
        past = tuple(gather_pages(c[l], page_table) for c in paged)
        mp, rp = mixing_sublayer(xp, pos_p, None, *mix_w, l)
        ms, rs = mixing_sublayer(xs, pos_s, past, *mix_w, l)
        for i in range(len(paged)):
            rows_p[i].append(rp[i])
            rows_s[i].append(rs[i])
        xp = layer_norm(DEEPNORM_ALPHA * xp + mp, ln_g[l, 0], ln_b[l, 0])
        xs = layer_norm(DEEPNORM_ALPHA * xs + ms, ln_g[l, 0], ln_b[l, 0])
        mk = jnp.einsum('bmd,dhe->bmhe', mem_prompt, w_xk[l])
        mv = jnp.einsum('bmd,dhe->bmhe', mem_prompt, w_xv[l])
        memk_p.append(mk)
        memv_p.append(mv)
        xp = layer_norm(DEEPNORM_ALPHA * xp + cross_sublayer(xp, mk, mv, w_xq[l], w_xo[l]), ln_g[l, 1], ln_b[l, 1])
        xs = layer_norm(DEEPNORM_ALPHA * xs + cross_sublayer(xs, cache_mem_k[l], cache_mem_v[l], w_xq[l], w_xo[l]),
                        ln_g[l, 1], ln_b[l, 1])
        fp, cp = conv_glu(xp, conv0, w_up[l], conv_w[l], conv_b[l], w_down[l])
        fs, cs = conv_glu(xs, state_ffn_conv[l], w_up[l], conv_w[l], conv_b[l], w_down[l])
        conv_p.append(cp)
        conv_s.append(cs)
        xp = layer_norm(DEEPNORM_ALPHA * xp + fp, ln_g[l, 2], ln_b[l, 2])
        xs = layer_norm(DEEPNORM_ALPHA * xs + fs, ln_g[l, 2], ln_b[l, 2])
    a_k_p, a_v_p, b_lat_p, b_pe_p, c_k_p, c_v_p = [jnp.stack(r, axis=0) for r in rows_p]
    a_k_s, a_v_s, b_lat_s, b_pe_s, c_k_s, c_v_s = [jnp.stack(r, axis=0) for r in rows_s]
    mem_k_p = jnp.stack(memk_p, axis=0)
    mem_v_p = jnp.stack(memv_p, axis=0)
    ffn_conv_p = jnp.stack(conv_p, axis=0)
    ffn_conv_s = jnp.stack(conv_s, axis=0)
    return (xp, xs, a_k_p, a_k_s, a_v_p, a_v_s, b_lat_p, b_lat_s, b_pe_p, b_pe_s,
            c_k_p, c_k_s, c_v_p, c_v_s, mem_k_p, mem_v_p, ffn_conv_p, ffn_conv_s)
```

```python
import functools
import math

import jax
import jax.numpy as jnp
import numpy as np
from jax import lax
from jax.experimental import pallas as pl
from jax.experimental.pallas import tpu as pltpu

F32 = jnp.float32
BF16 = jnp.bfloat16

D_MODEL = 1024
PAGE_SIZE = 128
A_HEADS = 4
A_QK_DIM = 64
A_V_DIM = 128
B_HEADS = 4
B_Q_RANK = 256
B_KV_RANK = 128
B_NOPE_DIM = 64
B_ROPE_DIM = 32
B_V_DIM = 64
ROPE_THETA = 10000.0
C_HEADS = 4
C_HEAD_DIM = 64
MOBA_BLOCK = 256
MOBA_TOPK = 3
IN_SIZES = (512, 128, 128, 256, 128, 32, 256, 64, 64)
N_MEM = 256
CROSS_HEADS = 4
CROSS_HEAD_DIM = 64
D_FF = 2816
LN_EPS = 1e-5
RMS_EPS = 1e-6
NEG_INF = -1e30
LANES = 128

COL_AQ = 0
COL_AK = 512
COL_AV = 640
COL_BCQ = 768
COL_BCKV = 1024
COL_BKPE = 1152
COL_CQ = 1280
COL_CK = 1536
COL_CV = 1600
IN_COLS_PACKED = 1664

A_SCALE = 1.0 / math.sqrt(A_QK_DIM)
B_SCALE = 1.0 / math.sqrt(B_NOPE_DIM + B_ROPE_DIM)
C_SCALE = 1.0 / math.sqrt(C_HEAD_DIM)
X_SCALE = 1.0 / math.sqrt(CROSS_HEAD_DIM)
_N_ALIBI = A_HEADS + C_HEADS
A_SLOPES = tuple(float(2.0 ** (-8.0 * (2 * g + 1) / _N_ALIBI)) for g in range(A_HEADS))
C_SLOPES = tuple(float(2.0 ** (-8.0 * (2 * g + 2) / _N_ALIBI)) for g in range(C_HEADS))

VMEM_LIMIT = 56 * 1024 * 1024


def _cparams(n_grid):
    return pltpu.CompilerParams(dimension_semantics=("arbitrary",) * n_grid, vmem_limit_bytes=VMEM_LIMIT)


def _layer_norm(y, g, b):
    mu = jnp.mean(y, axis=-1, keepdims=True)
    d = y - mu
    var = jnp.mean(d * d, axis=-1, keepdims=True)
    return d * lax.rsqrt(var + LN_EPS) * g + b


def _rms_norm(y, g):
    return y * lax.rsqrt(jnp.mean(y * y, axis=-1, keepdims=True) + RMS_EPS) * g


def _rope_tile(t, cos, sin_signed):
    lane = lax.broadcasted_iota(jnp.int32, t.shape, 1)
    first_half = (lane % B_ROPE_DIM) < (B_ROPE_DIM // 2)
    partner = jnp.where(first_half, pltpu.roll(t, LANES - B_ROPE_DIM // 2, 1), pltpu.roll(t, B_ROPE_DIM // 2, 1))
    return t * cos + partner * sin_signed


def _dot_nt(a, b):
    return lax.dot_general(a, b, (((1,), (1,)), ((), ())), preferred_element_type=F32)


def _dot(a, b):
    return jnp.dot(a, b, preferred_element_type=F32)


def _row_const(shape, rows_per_group, values):
    row = lax.broadcasted_iota(jnp.int32, shape, 0)
    grp = (row // rows_per_group) % len(values)
    out = jnp.full(shape, values[-1], F32)
    for i in range(len(values) - 1):
        out = jnp.where(grp == i, values[i], out)
    return out


def _inproj_kernel(x_ref, w_ref, wuq_ref, wuk_ref, gcq_ref, gckv_ref, cos_ref, sin_ref,
                   ak_ref, av_ref, blat_ref, bpe_ref, ck_ref, cv_ref,
                   aqm_ref, ka_ref, va_ref, qlat_ref, qpe_ref, kcat_ref, cq_ref, kc_ref, vc_ref,
                   *maybe_kmean_ref):
    tm = x_ref.shape[0]
    h = _dot(x_ref[...].astype(BF16), w_ref[...])
    cos = cos_ref[...]
    sin = sin_ref[...]

    aq = h[:, COL_AQ:COL_AQ + 512] * A_SCALE
    lane = lax.broadcasted_iota(jnp.int32, aq.shape, 1)
    is_map1 = (lane % (2 * A_QK_DIM)) < A_QK_DIM
    aqm_ref[0] = jnp.where(is_map1, aq, 0.0).astype(BF16)
    aqm_ref[1] = jnp.where(is_map1, 0.0, aq).astype(BF16)
    ak = h[:, COL_AK:COL_AK + 128]
    av = h[:, COL_AV:COL_AV + 128]
    ak_ref[...] = ak
    av_ref[...] = av
    ka_ref[...] = ak.astype(BF16)
    va_ref[...] = av.astype(BF16)

    cqn = _rms_norm(h[:, COL_BCQ:COL_BCQ + B_Q_RANK], gcq_ref[...])
    qb = _dot(cqn.astype(BF16), wuq_ref[...])
    qlat = _dot(qb[:, :B_HEADS * B_NOPE_DIM].astype(BF16), wuk_ref[...])
    qlat_ref[...] = (qlat * B_SCALE).astype(BF16)
    for hh in range(B_HEADS):
        lo = B_HEADS * B_NOPE_DIM + hh * LANES
        qpe_ref[:, hh * LANES:(hh + 1) * LANES] = (_rope_tile(qb[:, lo:lo + LANES], cos, sin) * B_SCALE).astype(BF16)
    blat = _rms_norm(h[:, COL_BCKV:COL_BCKV + B_KV_RANK], gckv_ref[...])
    blat_ref[...] = blat
    kpe = _rope_tile(h[:, COL_BKPE:COL_BKPE + LANES], cos, sin)
    bpe_ref[...] = kpe[:, :B_ROPE_DIM]
    kcat_ref[:, :B_KV_RANK] = blat.astype(BF16)
    kcat_ref[:, B_KV_RANK:] = kpe.astype(BF16)

    for g in range(C_HEADS):
        lo = COL_CQ + g * C_HEAD_DIM
        cq_ref[g] = (h[:, lo:lo + C_HEAD_DIM] * C_SCALE).astype(BF16)
    ck = h[:, COL_CK:COL_CK + C_HEAD_DIM]
    cv = h[:, COL_CV:COL_CV + C_HEAD_DIM]
    ck_ref[...] = ck
    cv_ref[...] = cv
    kc_ref[...] = ck.astype(BF16)
    vc_ref[...] = cv.astype(BF16)
    if maybe_kmean_ref:
        kmean_ref, = maybe_kmean_ref
        for j in range(tm // MOBA_BLOCK):
            kmean_ref[0, j:j + 1, :] = jnp.mean(ck[j * MOBA_BLOCK:(j + 1) * MOBA_BLOCK], axis=0, keepdims=True)


def _inproj(x, lw, l, cos, sin, tm, pos_tiles, with_kmean):
    t = x.shape[0]
    nt = t // tm
    row = lambda n: pl.BlockSpec((tm, n), lambda i: (i, 0))
    pos = pl.BlockSpec((tm, LANES), lambda i: (i % pos_tiles, 0))
    wspec = lambda a: pl.BlockSpec((None,) + a.shape[1:], lambda i: (l,) + (0,) * (a.ndim - 1))
    out_shape = [
        jax.ShapeDtypeStruct((t, 128), F32), jax.ShapeDtypeStruct((t, 128), F32),
        jax.ShapeDtypeStruct((t, 128), F32), jax.ShapeDtypeStruct((t, B_ROPE_DIM), F32),
        jax.ShapeDtypeStruct((t, C_HEAD_DIM), F32), jax.ShapeDtypeStruct((t, C_HEAD_DIM), F32),
        jax.ShapeDtypeStruct((2, t, 512), BF16), jax.ShapeDtypeStruct((t, 128), BF16),
        jax.ShapeDtypeStruct((t, 128), BF16), jax.ShapeDtypeStruct((t, 512), BF16),
        jax.ShapeDtypeStruct((t, 512), BF16), jax.ShapeDtypeStruct((t, 256), BF16),
        jax.ShapeDtypeStruct((C_HEADS, t, C_HEAD_DIM), BF16), jax.ShapeDtypeStruct((t, C_HEAD_DIM), BF16),
        jax.ShapeDtypeStruct((t, C_HEAD_DIM), BF16),
    ]
    out_specs = [
        row(128), row(128), row(128), row(B_ROPE_DIM), row(C_HEAD_DIM), row(C_HEAD_DIM),
        pl.BlockSpec((2, tm, 512), lambda i: (0, i, 0)), row(128), row(128), row(512), row(512), row(256),
        pl.BlockSpec((C_HEADS, tm, C_HEAD_DIM), lambda i: (0, i, 0)), row(C_HEAD_DIM), row(C_HEAD_DIM),
    ]
    if with_kmean:
        nb = tm // MOBA_BLOCK
        out_shape.append(jax.ShapeDtypeStruct((nt, nb, C_HEAD_DIM), F32))
        out_specs.append(pl.BlockSpec((1, nb, C_HEAD_DIM), lambda i: (i, 0, 0)))
    return pl.pallas_call(
        _inproj_kernel,
        grid=(nt,),
        in_specs=[row(D_MODEL), wspec(lw["w_in"]), wspec(lw["w_uq"]), wspec(lw["w_uk"]),
                  wspec(lw["g_cq"]), wspec(lw["g_ckv"]), pos, pos],
        out_specs=out_specs,
        out_shape=out_shape,
        compiler_params=_cparams(1),
        name="inproj",
    )(x, lw["w_in"], lw["w_uq"], lw["w_uk"], lw["g_cq"], lw["g_ckv"], cos, sin)


def _softmax_step(s, cc, v, m, l, acc):
    mx = jnp.max(s, axis=1, keepdims=True) + cc
    m_new = jnp.maximum(m, mx)
    alpha = jnp.exp(m - m_new)
    p = jnp.exp(s - (m_new - cc))
    l = alpha * l + jnp.sum(p, axis=1, keepdims=True)
    acc = alpha * acc + _dot(p.astype(BF16), v)
    return m_new, l, acc


def _attn_a_kernel(lam_init, aqm_ref, ka_ref, va_ref, lam_ref, gdiff_ref, o_ref):
    tq = aqm_ref.shape[1]
    tk = MOBA_BLOCK
    rows = 2 * A_HEADS * tq
    qi = pl.program_id(1)
    q0 = qi * tq
    lhs = jnp.concatenate([aqm_ref[mm, :, g * LANES:(g + 1) * LANES] for mm in range(2) for g in range(A_HEADS)], axis=0)
    slope = _row_const((rows, 1), tq, A_SLOPES)
    qrow = lax.broadcasted_iota(jnp.int32, (rows, tk), 0) % tq
    kcol = lax.broadcasted_iota(jnp.int32, (rows, tk), 1)
    rel = kcol - qrow
    bias = slope * rel.astype(F32)

    def chunk(c, carry, masked):
        start = pl.multiple_of(c * tk, tk)
        k = ka_ref[pl.ds(start, tk), :]
        v = va_ref[pl.ds(start, tk), :]
        s = _dot_nt(lhs, k) + bias
        off = c * tk - q0
        if masked:
            s = jnp.where(rel + off <= 0, s, NEG_INF)
        return _softmax_step(s, slope * jnp.asarray(off, F32), v, *carry)

    carry = (jnp.full((rows, 1), NEG_INF, F32), jnp.zeros((rows, 1), F32), jnp.zeros((rows, A_V_DIM), F32))
    n_full = q0 // tk
    carry = lax.fori_loop(0, n_full, lambda c, cr: chunk(c, cr, False), carry)
    _, l, acc = chunk(n_full, carry, True)

    lamv = lam_ref[...]
    lam = (jnp.exp(jnp.sum(lamv[0:1] * lamv[1:2], axis=1, keepdims=True))
           - jnp.exp(jnp.sum(lamv[2:3] * lamv[3:4], axis=1, keepdims=True)) + lam_init)
    o = acc / l
    half = rows // 2
    o = o[:half] - lam * o[half:]
    o = _rms_norm(o, gdiff_ref[...]) * (1.0 - lam_init)
    for g in range(A_HEADS):
        o_ref[:, g * A_V_DIM:(g + 1) * A_V_DIM] = o[g * tq:(g + 1) * tq]


def _attn_a(aqm, ka, va, lam4, gdiff, l, lam_init, nb, s, tq):
    t = ka.shape[0]
    nq = s // tq
    return pl.pallas_call(
        functools.partial(_attn_a_kernel, lam_init),
        grid=(nb, nq),
        in_specs=[pl.BlockSpec((2, tq, 512), lambda b, i: (0, b * nq + i, 0)),
                  pl.BlockSpec((s, 128), lambda b, i: (b, 0)),
                  pl.BlockSpec((s, 128), lambda b, i: (b, 0)),
                  pl.BlockSpec((None, 4, A_QK_DIM), lambda b, i: (l, 0, 0)),
                  pl.BlockSpec((None, 1, A_V_DIM), lambda b, i: (l, 0, 0))],
        out_specs=pl.BlockSpec((tq, 512), lambda b, i: (b * nq + i, 0)),
        out_shape=jax.ShapeDtypeStruct((t, 512), F32),
        compiler_params=_cparams(2),
        name="attn_a",
    )(aqm, ka, va, lam4, gdiff)


def _attn_b_kernel(qlat_ref, qpe_ref, kcat_ref, wuv_ref, o_ref):
    tq = qlat_ref.shape[0]
    tk = MOBA_BLOCK
    rows = B_HEADS * tq
    qi = pl.program_id(1)
    q0 = qi * tq
    lhs = jnp.concatenate(
        [jnp.concatenate([qlat_ref[:, hh * LANES:(hh + 1) * LANES], qpe_ref[:, hh * LANES:(hh + 1) * LANES]], axis=1)
         for hh in range(B_HEADS)], axis=0)
    qrow = lax.broadcasted_iota(jnp.int32, (rows, tk), 0) % tq
    kcol = lax.broadcasted_iota(jnp.int32, (rows, tk), 1)
    rel = kcol - qrow
    zero = jnp.zeros((rows, 1), F32)

    def chunk(c, carry, masked):
        start = pl.multiple_of(c * tk, tk)
        kc = kcat_ref[pl.ds(start, tk), :]
        s = _dot_nt(lhs, kc)
        if masked:
            s = jnp.where(rel + (c * tk - q0) <= 0, s, NEG_INF)
        return _softmax_step(s, zero, kc[:, :B_KV_RANK], *carry)

    carry = (jnp.full((rows, 1), NEG_INF, F32), jnp.zeros((rows, 1), F32), jnp.zeros((rows, B_KV_RANK), F32))
    n_full = q0 // tk
    carry = lax.fori_loop(0, n_full, lambda c, cr: chunk(c, cr, False), carry)
    _, l, acc = chunk(n_full, carry, True)
    o = acc / l
    o = jnp.concatenate([o[hh * tq:(hh + 1) * tq] for hh in range(B_HEADS)], axis=1)
    o_ref[...] = _dot(o.astype(BF16), wuv_ref[...])


def _attn_b(qlat, qpe, kcat, wuv, l, nb, s, tq):
    t = kcat.shape[0]
    nq = s // tq
    return pl.pallas_call(
        _attn_b_kernel,
        grid=(nb, nq),
        in_specs=[pl.BlockSpec((tq, 512), lambda b, i: (b * nq + i, 0)),
                  pl.BlockSpec((tq, 512), lambda b, i: (b * nq + i, 0)),
                  pl.BlockSpec((s, 256), lambda b, i: (b, 0)),
                  pl.BlockSpec((None, 512, 256), lambda b, i: (l, 0, 0))],
        out_specs=pl.BlockSpec((tq, 256), lambda b, i: (b * nq + i, 0)),
        out_shape=jax.ShapeDtypeStruct((t, 256), F32),
        compiler_params=_cparams(2),
        name="attn_b",
    )(qlat, qpe, kcat, wuv)


def _topk_select(gate, blk, n_valid, k):
    g = jnp.where(blk < n_valid, gate, NEG_INF)
    sel = jnp.zeros(gate.shape, F32)
    nb = gate.shape[1]
    for j in range(k):
        mx = jnp.max(g, axis=1, keepdims=True)
        idx = jnp.min(jnp.where(g == mx, blk, nb), axis=1, keepdims=True)
        pick = blk == idx
        counts = jnp.where(n_valid > j, 1.0, 0.0)
        sel = sel + jnp.where(pick, counts, 0.0)
        g = jnp.where(pick, -3e38, g)
    return sel


def _attn_c_kernel(cq_ref, kc_ref, vc_ref, kmean_ref, o_ref):
    tq = cq_ref.shape[1]
    tk = MOBA_BLOCK
    assert tq == tk
    rows = C_HEADS * tq
    nblk = kmean_ref.shape[0]
    qi = pl.program_id(1)
    lhs = cq_ref[...].reshape(rows, C_HEAD_DIM)
    gate = _dot_nt(lhs, kmean_ref[...].astype(BF16))
    blk = lax.broadcasted_iota(jnp.int32, (rows, nblk), 1)
    sel = _topk_select(gate, blk, qi, MOBA_TOPK)
    slope = _row_const((rows, 1), tq, C_SLOPES)
    qrow = lax.broadcasted_iota(jnp.int32, (rows, tk), 0) % tq
    kcol = lax.broadcasted_iota(jnp.int32, (rows, tk), 1)
    rel = kcol - qrow
    bias = slope * rel.astype(F32)

    def chunk(c, carry, own):
        start = pl.multiple_of(c * tk, tk)
        k = kc_ref[pl.ds(start, tk), :]
        v = vc_ref[pl.ds(start, tk), :]
        s = _dot_nt(lhs, k) + bias
        if own:
            s = jnp.where(rel <= 0, s, NEG_INF)
        else:
            chosen = jnp.max(jnp.where(blk == c, sel, 0.0), axis=1, keepdims=True)
            s = jnp.where(chosen > 0.0, s, NEG_INF)
        off = (c - qi) * tk
        return _softmax_step(s, slope * jnp.asarray(off, F32), v, *carry)

    carry = (jnp.full((rows, 1), NEG_INF, F32), jnp.zeros((rows, 1), F32), jnp.zeros((rows, C_HEAD_DIM), F32))
    carry = chunk(qi, carry, True)
    _, l, acc = lax.fori_loop(0, qi, lambda c, cr: chunk(c, cr, False), carry)
    o = acc / l
    o_ref[...] = jnp.concatenate([o[g * tq:(g + 1) * tq] for g in range(C_HEADS)], axis=1)


def _attn_c(cq, kc, vc, kmean, nb, s):
    t = kc.shape[0]
    tq = MOBA_BLOCK
    nq = s // tq
    return pl.pallas_call(
        _attn_c_kernel,
        grid=(nb, nq),
        in_specs=[pl.BlockSpec((C_HEADS, tq, C_HEAD_DIM), lambda b, i: (0, b * nq + i, 0)),
                  pl.BlockSpec((s, C_HEAD_DIM), lambda b, i: (b, 0)),
                  pl.BlockSpec((s, C_HEAD_DIM), lambda b, i: (b, 0)),
                  pl.BlockSpec((nq, C_HEAD_DIM), lambda b, i: (b, 0))],
        out_specs=pl.BlockSpec((tq, C_HEADS * C_HEAD_DIM), lambda b, i: (b * nq + i, 0)),
        out_shape=jax.ShapeDtypeStruct((t, C_HEADS * C_HEAD_DIM), F32),
        compiler_params=_cparams(2),
        name="attn_c",
    )(cq, kc, vc, kmean)


def _outproj_kernel(alpha, x_ref, oa_ref, ob_ref, oc_ref, w_ref, g_ref, b_ref, o_ref):
    y = (_dot(oa_ref[...].astype(BF16), w_ref[0:512, :])
         + _dot(ob_ref[...].astype(BF16), w_ref[512:768, :])
         + _dot(oc_ref[...].astype(BF16), w_ref[768:1024, :]))
    o_ref[...] = _layer_norm(alpha * x_ref[...] + y, g_ref[...], b_ref[...])


def _outproj(x, oa, ob, oc, lw, l, alpha, tm):
    t = x.shape[0]
    row = lambda n: pl.BlockSpec((tm, n), lambda i: (i, 0))
    return pl.pallas_call(
        functools.partial(_outproj_kernel, alpha),
        grid=(t // tm,),
        in_specs=[row(D_MODEL), row(512), row(256), row(256),
                  pl.BlockSpec((None, D_MODEL, D_MODEL), lambda i: (l, 0, 0)),
                  pl.BlockSpec((None, None, 1, D_MODEL), lambda i: (l, 0, 0, 0)),
                  pl.BlockSpec((None, None, 1, D_MODEL), lambda i: (l, 0, 0, 0))],
        out_specs=row(D_MODEL),
        out_shape=jax.ShapeDtypeStruct((t, D_MODEL), F32),
        compiler_params=_cparams(1),
        name="outproj",
    )(x, oa, ob, oc, lw["w_out"], lw["ln_g"], lw["ln_b"])


def _memkv_kernel(mem_ref, wk_ref, wv_ref, mk_ref, mv_ref):
    m = mem_ref[...].astype(BF16)
    mk_ref[...] = _dot(m, wk_ref[...])
    mv_ref[...] = _dot(m, wv_ref[...])


def _memkv(mem, lw, l):
    nb = mem.shape[0]
    hd = CROSS_HEADS * CROSS_HEAD_DIM
    wspec = pl.BlockSpec((None, D_MODEL, hd), lambda b: (l, 0, 0))
    ospec = pl.BlockSpec((None, N_MEM, hd), lambda b: (b, 0, 0))
    return pl.pallas_call(
        _memkv_kernel,
        grid=(nb,),
        in_specs=[pl.BlockSpec((None, N_MEM, D_MODEL), lambda b: (b, 0, 0)), wspec, wspec],
        out_specs=[ospec, ospec],
        out_shape=[jax.ShapeDtypeStruct((nb, N_MEM, hd), F32)] * 2,
        compiler_params=_cparams(1),
        name="memkv",
    )(mem, lw["w_xk"], lw["w_xv"])


def _cross_heads(q, mk, mv):
    lane_head = lax.broadcasted_iota(jnp.int32, q.shape, 1) // CROSS_HEAD_DIM
    o = jnp.zeros(q.shape, F32)
    for hh in range(CROSS_HEADS):
        mine = lane_head == hh
        s = _dot_nt(jnp.where(mine, q, 0.0).astype(BF16), mk)
        p = jnp.exp(s - jnp.max(s, axis=1, keepdims=True))
        p = p / jnp.sum(p, axis=1, keepdims=True)
        o = jnp.where(mine, _dot(p.astype(BF16), mv), o)
    return o


def _cross_kernel(alpha, x_ref, mk_ref, mv_ref, wq_ref, wo_ref, g_ref, b_ref, o_ref):
    x = x_ref[...]
    q = _dot(x.astype(BF16), wq_ref[...]) * X_SCALE
    o = _cross_heads(q, mk_ref[...].astype(BF16), mv_ref[...].astype(BF16))
    y = _dot(o.astype(BF16), wo_ref[...])
    o_ref[...] = _layer_norm(alpha * x + y, g_ref[...], b_ref[...])


def _cross(x, mk, mv, lw, l, alpha, nb, s, tm):
    t = x.shape[0]
    nt = s // tm
    hd = CROSS_HEADS * CROSS_HEAD_DIM
    row = pl.BlockSpec((tm, D_MODEL), lambda b, i: (b * nt + i, 0))
    mspec = pl.BlockSpec((None, N_MEM, hd), lambda b, i: (b, 0, 0))
    return pl.pallas_call(
        functools.partial(_cross_kernel, alpha),
        grid=(nb, nt),
        in_specs=[row, mspec, mspec,
                  pl.BlockSpec((None, D_MODEL, hd), lambda b, i: (l, 0, 0)),
                  pl.BlockSpec((None, hd, D_MODEL), lambda b, i: (l, 0, 0)),
                  pl.BlockSpec((None, None, 1, D_MODEL), lambda b, i: (l, 1, 0, 0)),
                  pl.BlockSpec((None, None, 1, D_MODEL), lambda b, i: (l, 1, 0, 0))],
        out_specs=row,
        out_shape=jax.ShapeDtypeStruct((t, D_MODEL), F32),
        compiler_params=_cparams(2),
        name="cross",
    )(x, mk, mv, lw["w_xq"], lw["w_xo"], lw["ln_g"], lw["ln_b"])


def _cross_dec_kernel(alpha, x_ref, mk_ref, mv_ref, wq_ref, wo_ref, g_ref, b_ref, o_ref):
    x = x_ref[...]
    n = x.shape[0]
    hd = CROSS_HEADS * CROSS_HEAD_DIM
    q = _dot(x.astype(BF16), wq_ref[...]) * X_SCALE
    row = lax.broadcasted_iota(jnp.int32, (8, hd), 0)
    lane_head = lax.broadcasted_iota(jnp.int32, (8, hd), 1) // CROSS_HEAD_DIM
    mine = row == lane_head
    outs = []
    for j in range(n):
        lhs = jnp.where(mine, jnp.broadcast_to(q[j:j + 1], (8, hd)), 0.0).astype(BF16)
        s = _dot(lhs, mk_ref[j].astype(BF16))
        p = jnp.exp(s - jnp.max(s, axis=1, keepdims=True))
        p = p / jnp.sum(p, axis=1, keepdims=True)
        o8 = _dot_nt(p.astype(BF16), mv_ref[j].astype(BF16))
        outs.append(jnp.sum(jnp.where(mine, o8, 0.0), axis=0, keepdims=True))
    o = jnp.concatenate(outs, axis=0)
    y = _dot(o.astype(BF16), wo_ref[...])
    o_ref[...] = _layer_norm(alpha * x + y, g_ref[...], b_ref[...])


def _cross_dec(x, mem_k, mem_v, lw, l, alpha):
    t = x.shape[0]
    tm = 8
    hd = CROSS_HEADS * CROSS_HEAD_DIM
    row = pl.BlockSpec((tm, D_MODEL), lambda i: (i, 0))
    mspec = pl.BlockSpec((None, tm, hd, N_MEM), lambda i: (l, i, 0, 0))
    return pl.pallas_call(
        functools.partial(_cross_dec_kernel, alpha),
        grid=(t // tm,),
        in_specs=[row, mspec, mspec,
                  pl.BlockSpec((None, D_MODEL, hd), lambda i: (l, 0, 0)),
                  pl.BlockSpec((None, hd, D_MODEL), lambda i: (l, 0, 0)),
                  pl.BlockSpec((None, None, 1, D_MODEL), lambda i: (l, 1, 0, 0)),
                  pl.BlockSpec((None, None, 1, D_MODEL), lambda i: (l, 1, 0, 0))],
        out_specs=row,
        out_shape=jax.ShapeDtypeStruct((t, D_MODEL), F32),
        compiler_params=_cparams(1),
        name="cross_dec",
    )(x, mem_k, mem_v, lw["w_xq"], lw["w_xo"], lw["ln_g"], lw["ln_b"])


FFN_TF = 1408
FFN_NF = D_FF // FFN_TF


def _gelu(x):
    return 0.5 * x * (1.0 + lax.erf(x * (1.0 / math.sqrt(2.0))))


def _ffn_finish(alpha, j, x, y_blk, wd_ref, g_ref, b_ref, o_ref, acc_ref):
    part = _dot(y_blk.astype(BF16), wd_ref[...])

    @pl.when(j == 0)
    def _():
        acc_ref[...] = part

    @pl.when(j > 0)
    def _():
        acc_ref[...] += part

    @pl.when(j == FFN_NF - 1)
    def _():
        o_ref[...] = _layer_norm(alpha * x + acc_ref[...], g_ref[...], b_ref[...])


def _ffn_prompt_kernel(alpha, tiles_per_seq, x_ref, wg_ref, wv_ref, cw_ref, cb_ref, wd_ref, g_ref, b_ref,
                       o_ref, tail_ref, acc_ref, carry_ref):
    i = pl.program_id(0)
    j = pl.program_id(1)
    x = x_ref[...]
    tm = x.shape[0]
    xb = x.astype(BF16)
    gate = _dot(xb, wg_ref[...])
    val = _dot(xb, wv_ref[...])
    @pl.when(i % tiles_per_seq == 0)
    def _():
        carry_ref[j] = jnp.zeros(carry_ref.shape[1:], F32)

    prev = carry_ref[j]
    row = lax.broadcasted_iota(jnp.int32, gate.shape, 0)
    g1 = jnp.where(row == 0, prev[7:8], pltpu.roll(gate, 1, 0))
    g2 = jnp.where(row == 0, prev[6:7], jnp.where(row == 1, prev[7:8], pltpu.roll(gate, 2, 0)))
    cw = cw_ref[...]
    conv = g2 * cw[0:1] + g1 * cw[1:2] + gate * cw[2:3] + cb_ref[...]
    tail = gate[tm - 8:tm]
    carry_ref[j] = tail
    tail_ref[...] = tail
    _ffn_finish(alpha, j, x, _gelu(conv) * val, wd_ref, g_ref, b_ref, o_ref, acc_ref)


def _ffn_dec_kernel(alpha, x_ref, s0_ref, s1_ref, wg_ref, wv_ref, cw_ref, cb_ref, wd_ref, g_ref, b_ref,
                    o_ref, gout_ref, acc_ref):
    j = pl.program_id(1)
    x = x_ref[...]
    xb = x.astype(BF16)
    gate = _dot(xb, wg_ref[...])
    val = _dot(xb, wv_ref[...])
    cw = cw_ref[...]
    conv = s0_ref[...] * cw[0:1] + s1_ref[...] * cw[1:2] + gate * cw[2:3] + cb_ref[...]
    gout_ref[...] = gate
    _ffn_finish(alpha, j, x, _gelu(conv) * val, wd_ref, g_ref, b_ref, o_ref, acc_ref)


def _ffn_specs(l, tm):
    return [pl.BlockSpec((None, D_MODEL, FFN_TF), lambda i, j: (l, 0, j)),
            pl.BlockSpec((None, D_MODEL, FFN_TF), lambda i, j: (l, 0, FFN_NF + j)),
            pl.BlockSpec((None, 3, FFN_TF), lambda i, j: (l, 0, j)),
            pl.BlockSpec((None, 1, FFN_TF), lambda i, j: (l, 0, j)),
            pl.BlockSpec((None, FFN_TF, D_MODEL), lambda i, j: (l, j, 0)),
            pl.BlockSpec((None, None, 1, D_MODEL), lambda i, j: (l, 2, 0, 0)),
            pl.BlockSpec((None, None, 1, D_MODEL), lambda i, j: (l, 2, 0, 0))]


def _ffn_prompt(x, lw, l, alpha, nb, s, tm):
    t = x.shape[0]
    tps = s // tm
    row = pl.BlockSpec((tm, D_MODEL), lambda i, j: (i, 0))
    y, tail = pl.pallas_call(
        functools.partial(_ffn_prompt_kernel, alpha, tps),
        grid=(t // tm, FFN_NF),
        in_specs=[row] + _ffn_specs(l, tm),
        out_specs=[row, pl.BlockSpec((None, 8, FFN_TF), lambda i, j: (i, 0, j))],
        out_shape=[jax.ShapeDtypeStruct((t, D_MODEL), F32), jax.ShapeDtypeStruct((t // tm, 8, D_FF), F32)],
        scratch_shapes=[pltpu.VMEM((tm, D_MODEL), F32), pltpu.VMEM((FFN_NF, 8, FFN_TF), F32)],
        compiler_params=_cparams(2),
        name="ffn",
    )(x, lw["w_up"], lw["w_up"], lw["conv_w"], lw["conv_b"], lw["w_down"], lw["ln_g"], lw["ln_b"])
    return y, tail[tps - 1::tps, 6:8]


def _ffn_dec(x, s0, s1, lw, l, alpha):
    t = x.shape[0]
    row = pl.BlockSpec((t, D_MODEL), lambda i, j: (i, 0))
    fcol = pl.BlockSpec((t, FFN_TF), lambda i, j: (i, j))
    return pl.pallas_call(
        functools.partial(_ffn_dec_kernel, alpha),
        grid=(1, FFN_NF),
        in_specs=[row, fcol, fcol] + _ffn_specs(l, t),
        out_specs=[row, fcol],
        out_shape=[jax.ShapeDtypeStruct((t, D_MODEL), F32), jax.ShapeDtypeStruct((t, D_FF), F32)],
        scratch_shapes=[pltpu.VMEM((t, D_MODEL), F32)],
        compiler_params=_cparams(2),
        name="ffn_dec",
    )(x, s0, s1, lw["w_up"], lw["w_up"], lw["conv_w"], lw["conv_b"], lw["w_down"], lw["ln_g"], lw["ln_b"])


DEC_CHUNK_PAGES = 16
DEC_CHUNK = DEC_CHUNK_PAGES * PAGE_SIZE
N_CACHES = 6


def _dec_attn_kernel(layer, lam_init, nseq, npages,
                     pt_ref,
                     qa_ref, qlat_ref, qpe_ref, qc_ref,
                     ka_new_ref, va_new_ref, kcat_new_ref, kc_new_ref, vc_new_ref,
                     lam_ref, gdiff_ref, wuv_ref,
                     cak_ref, cav_ref, cbl_ref, cbr_ref, cck_ref, ccv_ref,
                     oa_ref, ob_ref, oc_ref,
                     bak, bav, bbl, bbr, bck, bcv, sems, gate_s, m_s, l_s, acc_s):
    b = pl.program_id(0)
    nch = npages // DEC_CHUNK_PAGES
    past = npages * PAGE_SIZE
    blocks_per_chunk = DEC_CHUNK // MOBA_BLOCK
    caches = (cak_ref, cav_ref, cbl_ref, cbr_ref, cck_ref, ccv_ref)
    bufs = (bak, bav, bbl, bbr, bck, bcv)

    def start_chunk(seq, c, slot):
        for p in range(DEC_CHUNK_PAGES):
            page = pt_ref[seq, c * DEC_CHUNK_PAGES + p]
            for a in range(N_CACHES):
                pltpu.make_async_copy(caches[a].at[layer, page], bufs[a].at[slot, p], sems.at[slot, a]).start()

    def wait_chunk(slot):
        for a in range(N_CACHES):
            pltpu.make_async_copy(caches[a].at[layer, pl.ds(0, DEC_CHUNK_PAGES)], bufs[a].at[slot],
                                  sems.at[slot, a]).wait()

    @pl.when(b == 0)
    def _():
        start_chunk(0, 0, 0)

    qa = qa_ref[...]
    qlat = qlat_ref[...]
    qpe = qpe_ref[...]
    qc = qc_ref[...]
    slope_a = _row_const((8, 1), 1, A_SLOPES)
    slope_c = _row_const((8, 1), 1, C_SLOPES)
    kcol = lax.broadcasted_iota(jnp.int32, (8, DEC_CHUNK), 1).astype(F32)

    def chunk_body(c, carry):
        slot = c % 2
        nxt = c + 1

        @pl.when(nxt < nch)
        def _():
            start_chunk(b, nxt, 1 - slot)

        @pl.when(jnp.logical_and(nxt == nch, b + 1 < nseq))
        def _():
            start_chunk(b + 1, 0, 1 - slot)

        wait_chunk(slot)
        (ma, la, acca), (mb, lb, accb) = carry
        dist = jnp.asarray(past - c * DEC_CHUNK, F32) - kcol
        ka = bak[slot].reshape(DEC_CHUNK, 128).astype(BF16)
        va = bav[slot].reshape(DEC_CHUNK, 128).astype(BF16)
        s = _dot_nt(qa, ka) - slope_a * dist
        zero = jnp.zeros((8, 1), F32)
        ca = _softmax_step(s, zero, va, ma, la, acca)
        lat = bbl[slot].reshape(DEC_CHUNK, B_KV_RANK).astype(BF16)
        s_pe = jnp.concatenate([_dot(qpe, bbr[slot, p].astype(BF16)) for p in range(DEC_CHUNK_PAGES)], axis=1)
        s = _dot_nt(qlat, lat) + s_pe
        cb = _softmax_step(s, zero, lat, mb, lb, accb)
        pages_per_block = MOBA_BLOCK // PAGE_SIZE
        for j in range(blocks_per_chunk):
            lo = j * MOBA_BLOCK
            n = c * blocks_per_chunk + j
            pages = range(j * pages_per_block, (j + 1) * pages_per_block)
            s_raw = jnp.concatenate([_dot(qc, bck[slot, p].astype(BF16)) for p in pages], axis=1)
            gate = jnp.sum(s_raw, axis=1, keepdims=True) * (1.0 / MOBA_BLOCK)
            sj = s_raw - slope_c * dist[:, lo:lo + MOBA_BLOCK]
            mj = jnp.max(sj, axis=1, keepdims=True)
            pj = jnp.exp(sj - mj)
            pjb = pj.astype(BF16)
            acc = jnp.zeros((8, C_HEAD_DIM), F32)
            for i, p in enumerate(pages):
                acc = acc + _dot_nt(pjb[:, i * PAGE_SIZE:(i + 1) * PAGE_SIZE], bcv[slot, p].astype(BF16))
            gate_s[n] = jnp.broadcast_to(gate, (8, LANES))
            m_s[n] = jnp.broadcast_to(mj, (8, LANES))
            l_s[n] = jnp.broadcast_to(jnp.sum(pj, axis=1, keepdims=True), (8, LANES))
            acc_s[n] = acc
        return ca, cb

    init = (jnp.full((8, 1), NEG_INF, F32), jnp.zeros((8, 1), F32), jnp.zeros((8, 128), F32))
    (ma, la, acca), (mb, lb, accb) = lax.fori_loop(0, nch, chunk_body, (init, init))

    def add_new_key(s_new, v_new, m, l, acc):
        m_f = jnp.maximum(m, s_new)
        w_old = jnp.exp(m - m_f)
        w_new = jnp.exp(s_new - m_f)
        return (acc * w_old + w_new * v_new) / (l * w_old + w_new)

    s_new = jnp.sum(qa.astype(F32) * ka_new_ref[...].astype(F32), axis=1, keepdims=True)
    o = add_new_key(s_new, va_new_ref[...].astype(F32), ma, la, acca)
    lamv = lam_ref[...]
    lam = (jnp.exp(jnp.sum(lamv[0:1] * lamv[1:2], axis=1, keepdims=True))
           - jnp.exp(jnp.sum(lamv[2:3] * lamv[3:4], axis=1, keepdims=True)) + lam_init)
    oa_ref[...] = _rms_norm(o[:A_HEADS] - lam * o[A_HEADS:], gdiff_ref[...]) * (1.0 - lam_init)
    kcat_new = kcat_new_ref[...].astype(F32)
    s_new = (jnp.sum(qlat.astype(F32) * kcat_new[:, :B_KV_RANK], axis=1, keepdims=True)
             + jnp.sum(qpe.astype(F32) * kcat_new[:, B_KV_RANK:B_KV_RANK + B_ROPE_DIM], axis=1, keepdims=True))
    o = add_new_key(s_new, kcat_new[:, :B_KV_RANK], mb, lb, accb)
    y = _dot(o.astype(BF16), wuv_ref[...])
    mine = (lax.broadcasted_iota(jnp.int32, y.shape, 0) == lax.broadcasted_iota(jnp.int32, y.shape, 1) // B_V_DIM)
    ob_ref[...] = jnp.sum(jnp.where(mine, y, 0.0), axis=0, keepdims=True)
    nblk = past // MOBA_BLOCK
    g = gate_s[...]
    blk = lax.broadcasted_iota(jnp.int32, g.shape, 0)
    sel = jnp.zeros(g.shape, F32)
    for _ in range(MOBA_TOPK):
        mx = jnp.max(g, axis=0, keepdims=True)
        idx = jnp.min(jnp.where(g == mx, blk, nblk), axis=0, keepdims=True)
        pick = blk == idx
        sel = jnp.where(pick, 1.0, sel)
        g = jnp.where(pick, -3e38, g)
    s_own = jnp.sum(qc.astype(F32) * kc_new_ref[...].astype(F32), axis=1, keepdims=True)
    mblk = m_s[...]
    m_f = jnp.maximum(jnp.max(jnp.where(sel > 0.0, mblk, NEG_INF), axis=0), s_own)
    w = sel * jnp.exp(jnp.minimum(mblk - m_f, 0.0))
    w_own = jnp.exp(s_own - m_f)
    den = jnp.sum(w * l_s[...], axis=0) + w_own
    num = jnp.sum(w[:, :, :C_HEAD_DIM] * acc_s[...], axis=0) + w_own[:, :C_HEAD_DIM] * vc_new_ref[...].astype(F32)
    oc_ref[...] = (num / den[:, :C_HEAD_DIM])[:C_HEADS]


def _dec_attn(page_table, qa, qlat, qpe, qc, ka_new, va_new, kcat_new, kc_new, vc_new, lw, caches, l, lam_init):
    nseq, npages = page_table.shape
    assert npages % DEC_CHUNK_PAGES == 0 and (npages * PAGE_SIZE) // MOBA_BLOCK >= MOBA_TOPK
    nblk = npages * PAGE_SIZE // MOBA_BLOCK
    seq = lambda r, n: pl.BlockSpec((None, r, n), lambda b, pt: (b, 0, 0))
    wspec = lambda a: pl.BlockSpec((None,) + a.shape[1:], lambda b, pt: (l,) + (0,) * (a.ndim - 1))
    anyspec = pl.BlockSpec(memory_space=pl.ANY)
    grid_spec = pltpu.PrefetchScalarGridSpec(
        num_scalar_prefetch=1,
        grid=(nseq,),
        in_specs=[seq(8, 128), seq(8, 128), seq(8, B_ROPE_DIM), seq(8, C_HEAD_DIM),
                  seq(1, 128), seq(1, 128), seq(1, 256), seq(1, C_HEAD_DIM), seq(1, C_HEAD_DIM),
                  wspec(lw["lam4"]), wspec(lw["g_diff"]), wspec(lw["w_uv_cat"])] + [anyspec] * N_CACHES,
        out_specs=[seq(A_HEADS, A_V_DIM), seq(1, B_HEADS * B_V_DIM), seq(C_HEADS, C_HEAD_DIM)],
        scratch_shapes=[pltpu.VMEM((2, DEC_CHUNK_PAGES, PAGE_SIZE, 128), F32),
                        pltpu.VMEM((2, DEC_CHUNK_PAGES, PAGE_SIZE, 128), F32),
                        pltpu.VMEM((2, DEC_CHUNK_PAGES, PAGE_SIZE, B_KV_RANK), F32),
                        pltpu.VMEM((2, DEC_CHUNK_PAGES, B_ROPE_DIM, PAGE_SIZE), F32),
                        pltpu.VMEM((2, DEC_CHUNK_PAGES, C_HEAD_DIM, PAGE_SIZE), F32),
                        pltpu.VMEM((2, DEC_CHUNK_PAGES, C_HEAD_DIM, PAGE_SIZE), F32),
                        pltpu.SemaphoreType.DMA((2, N_CACHES)),
                        pltpu.VMEM((nblk, 8, LANES), F32), pltpu.VMEM((nblk, 8, LANES), F32),
                        pltpu.VMEM((nblk, 8, LANES), F32), pltpu.VMEM((nblk, 8, C_HEAD_DIM), F32)],
    )
    return pl.pallas_call(
        functools.partial(_dec_attn_kernel, l, lam_init, nseq, npages),
        grid_spec=grid_spec,
        out_shape=[jax.ShapeDtypeStruct((nseq, A_HEADS, A_V_DIM), F32),
                   jax.ShapeDtypeStruct((nseq, 1, B_HEADS * B_V_DIM), F32),
                   jax.ShapeDtypeStruct((nseq, C_HEADS, C_HEAD_DIM), F32)],
        compiler_params=_cparams(1),
        name="dec_attn",
    )(page_table, qa, qlat, qpe, qc, ka_new, va_new, kcat_new, kc_new, vc_new,
      lw["lam4"], lw["g_diff"], lw["w_uv_cat"], *caches)


def _rope_tables(pos):
    half = B_ROPE_DIM // 2
    freqs = ROPE_THETA ** (-jnp.arange(half, dtype=F32) / half)
    ang = pos.astype(F32)[:, None] * freqs[None, :]
    reps = LANES // half
    cos = jnp.tile(jnp.cos(ang), (1, reps))
    sign = jnp.where((jnp.arange(LANES) % B_ROPE_DIM) < half, -1.0, 1.0).astype(F32)
    sin = jnp.tile(jnp.sin(ang), (1, reps)) * sign[None, :]
    return cos, sin


def _pack_weights(w_in, w_uq, g_cq, g_ckv, w_uk, w_uv, lam_q1, lam_k1, lam_q2, lam_k2, g_diff, w_out,
                  w_xq, w_xk, w_xv, w_xo, w_up, conv_w, conv_b, w_down, ln_g, ln_b):
    depth = w_in.shape[0]
    parts = jnp.split(w_in, list(np.cumsum(IN_SIZES)[:-1]), axis=-1)
    a_q, a_k, a_v, b_cq, b_ckv, b_kpe, c_q, c_k, c_v = parts
    pad = jnp.zeros((depth, D_MODEL, LANES - B_ROPE_DIM), w_in.dtype)
    w_in_p = jnp.concatenate([a_q, a_k, a_v, b_cq, b_ckv, b_kpe, pad, c_q, c_k, c_v], axis=-1)
    assert w_in_p.shape[-1] == IN_COLS_PACKED
    nope = w_uq[..., :B_NOPE_DIM].reshape(depth, B_Q_RANK, B_HEADS * B_NOPE_DIM)
    rope = jnp.pad(w_uq[..., B_NOPE_DIM:], ((0, 0), (0, 0), (0, 0), (0, LANES - B_ROPE_DIM)))
    w_uq_p = jnp.concatenate([nope, rope.reshape(depth, B_Q_RANK, B_HEADS * LANES)], axis=-1)
    eye = jnp.eye(B_HEADS, dtype=w_uk.dtype)
    w_uk_bd = jnp.einsum('lrhn,hg->lhngr', w_uk, eye).reshape(depth, B_HEADS * B_NOPE_DIM, B_HEADS * B_KV_RANK)
    w_uv_bd = jnp.einsum('lrhv,hg->lhrgv', w_uv, eye).reshape(depth, B_HEADS * B_KV_RANK, B_HEADS * B_V_DIM)
    hd = CROSS_HEADS * CROSS_HEAD_DIM
    return {
        "w_in": w_in_p.astype(BF16), "w_uq": w_uq_p.astype(BF16), "w_uk": w_uk_bd.astype(BF16),
        "g_cq": g_cq[:, None, :], "g_ckv": g_ckv[:, None, :],
        "w_uv_bd": w_uv_bd.astype(BF16),
        "w_uv_cat": w_uv.reshape(depth, B_KV_RANK, B_HEADS * B_V_DIM).astype(BF16),
        "lam4": jnp.stack([lam_q1, lam_k1, lam_q2, lam_k2], axis=1), "g_diff": g_diff[:, None, :],
        "w_out": w_out.astype(BF16),
        "w_xq": w_xq.reshape(depth, D_MODEL, hd).astype(BF16), "w_xk": w_xk.reshape(depth, D_MODEL, hd).astype(BF16),
        "w_xv": w_xv.reshape(depth, D_MODEL, hd).astype(BF16), "w_xo": w_xo.reshape(depth, hd, D_MODEL).astype(BF16),
        "w_up": w_up.astype(BF16), "conv_w": conv_w, "conv_b": conv_b[:, None, :], "w_down": w_down.astype(BF16),
        "ln_g": ln_g[:, :, None, :], "ln_b": ln_b[:, :, None, :],
    }


def _pad_rows(a, rows):
    return jnp.pad(a, ((0, 0), (0, rows - a.shape[1]), (0, 0)))


def kernel(x_prompt, x_sample, cache_a_k, cache_a_v, cache_b_latent, cache_b_rope, cache_c_k, cache_c_v, cache_mem_k, cache_mem_v, state_ffn_conv, page_table, mem_prompt, w_in, w_uq, g_cq, g_ckv, w_uk, w_uv, lam_q1, lam_k1, lam_q2, lam_k2, g_diff, w_out, w_xq, w_xk, w_xv, w_xo, w_up, conv_w, conv_b, w_down, ln_g, ln_b):
    nb, s, _ = x_prompt.shape
    nd = x_sample.shape[0]
    depth = w_in.shape[0]
    npages = page_table.shape[1]
    past = npages * PAGE_SIZE
    alpha = float((2 * depth) ** 0.25)
    hd = CROSS_HEADS * CROSS_HEAD_DIM
    assert x_sample.shape[1] == 1 and s % 512 == 0 and nd % 8 == 0

    lw = _pack_weights(w_in, w_uq, g_cq, g_ckv, w_uk, w_uv, lam_q1, lam_k1, lam_q2, lam_k2, g_diff, w_out,
                       w_xq, w_xk, w_xv, w_xo, w_up, conv_w, conv_b, w_down, ln_g, ln_b)
    cos_p, sin_p = _rope_tables(jnp.arange(s, dtype=jnp.int32))
    cos_s, sin_s = _rope_tables(jnp.full((nd,), past, jnp.int32))
    n_pool = cache_a_k.shape[1]
    caches = (cache_a_k.reshape(depth, n_pool, PAGE_SIZE, 128), cache_a_v.reshape(depth, n_pool, PAGE_SIZE, 128),
              cache_b_latent, cache_b_rope.transpose(0, 1, 3, 2),
              cache_c_k.transpose(0, 1, 3, 4, 2).reshape(depth, n_pool, C_HEAD_DIM, PAGE_SIZE),
              cache_c_v.transpose(0, 1, 3, 4, 2).reshape(depth, n_pool, C_HEAD_DIM, PAGE_SIZE))
    mem_k = cache_mem_k.transpose(0, 1, 3, 4, 2).reshape(depth, nd, hd, N_MEM)
    mem_v = cache_mem_v.transpose(0, 1, 3, 4, 2).reshape(depth, nd, hd, N_MEM)

    xp = x_prompt.reshape(nb * s, D_MODEL)
    xs = x_sample.reshape(nd, D_MODEL)
    tm = 512
    rows_p = [[] for _ in range(6)]
    rows_s = [[] for _ in range(6)]
    memk_p, memv_p, conv_p, conv_s = [], [], [], []
    for l in range(depth):
        lam_init = 0.8 - 0.6 * math.exp(-0.3 * l)
        (ak, av, blat, bpe, ck, cv, aqm, ka, va, qlat, qpe, kcat, cq, kc, vc, kmean) = _inproj(
            xp, lw, l, cos_p, sin_p, tm, s // tm, True)
        for i, r in enumerate((ak, av, blat, bpe, ck, cv)):
            rows_p[i].append(r)
        oa = _attn_a(aqm, ka, va, lw["lam4"], lw["g_diff"], l, lam_init, nb, s, 128)
        ob = _attn_b(qlat, qpe, kcat, lw["w_uv_bd"], l, nb, s, 256)
        oc = _attn_c(cq, kc, vc, kmean.reshape(nb * s // MOBA_BLOCK, C_HEAD_DIM), nb, s)
        xp = _outproj(xp, oa, ob, oc, lw, l, alpha, tm)
        mk, mv = _memkv(mem_prompt, lw, l)
        memk_p.append(mk)
        memv_p.append(mv)
        xp = _cross(xp, mk, mv, lw, l, alpha, nb, s, tm)
        xp, tail = _ffn_prompt(xp, lw, l, alpha, nb, s, tm)
        conv_p.append(tail)
        (ak, av, blat, bpe, ck, cv, aqm, ka, va, qlat, qpe, kcat, cq, kc, vc) = _inproj(
            xs, lw, l, cos_s, sin_s, nd, 1, False)
        for i, r in enumerate((ak, av, blat, bpe, ck, cv)):
            rows_s[i].append(r)
        qa = aqm.reshape(2, nd, A_HEADS, 128).transpose(1, 0, 2, 3).reshape(nd, 2 * A_HEADS, 128)
        qlat8 = _pad_rows(qlat.reshape(nd, B_HEADS, 128), 8)
        qpe8 = _pad_rows(qpe.reshape(nd, B_HEADS, 128)[:, :, :B_ROPE_DIM], 8)
        qc8 = _pad_rows(cq.transpose(1, 0, 2), 8)
        oa, ob, oc = _dec_attn(page_table, qa, qlat8, qpe8, qc8, ka[:, None], va[:, None], kcat[:, None],
                               kc[:, None], vc[:, None], lw, caches, l, lam_init)
        xs = _outproj(xs, oa.reshape(nd, 512), ob.reshape(nd, 256), oc.reshape(nd, 256), lw, l, alpha, nd)
        xs = _cross_dec(xs, mem_k, mem_v, lw, l, alpha)
        st = state_ffn_conv[l]
        xs, gnew = _ffn_dec(xs, st[:, 0], st[:, 1], lw, l, alpha)
        conv_s.append(jnp.stack([st[:, 1], gnew], axis=1))

    def stack(rs, shape):
        return jnp.stack(rs, axis=0).reshape((depth,) + shape)

    tail_dims = ((1, 128), (1, 128), (B_KV_RANK,), (B_ROPE_DIM,), (1, C_HEAD_DIM), (1, C_HEAD_DIM))
    outs = [xp.reshape(nb, s, D_MODEL), xs.reshape(nd, 1, D_MODEL)]
    for i in range(6):
        outs.append(stack(rows_p[i], (nb, s) + tail_dims[i]))
        outs.append(stack(rows_s[i], (nd, 1) + tail_dims[i]))
    outs.append(stack(memk_p, (nb, N_MEM, CROSS_HEADS, CROSS_HEAD_DIM)))
    outs.append(stack(memv_p, (nb, N_MEM, CROSS_HEADS, CROSS_HEAD_DIM)))
    outs.append(jnp.stack(conv_p, axis=0))
    outs.append(jnp.stack(conv_s, axis=0))
    return tuple(outs)
```
